```python
import jax, jax.numpy as jnp
from jax import lax
import numpy as np

D_MODEL = 2048
BATCH = 8
SEQ = 2048
DEPTH = 1
DEC_BATCH = 128
DEC_SEQ = 4
PAST_LEN = 2048
PAGE_SIZE = 128

N_HEADS = 8
N_KV_HEADS = 2
HEAD_DIM = 128
ATTN_W = N_HEADS * HEAD_DIM
KV_W = N_KV_HEADS * HEAD_DIM
IDX_HEADS = 16
IDX_DIM = 64
TOPK_MAX = 256
ATTN_BLOCK = 128
ATTN_SCALE = HEAD_DIM ** -0.5
IDX_SCALE = IDX_DIM ** -0.5
LRU_W = D_MODEL // 2
LRU_BLOCKS = 8
LRU_BW = LRU_W // LRU_BLOCKS
CONV_W = 4
LRU_C = 8.0
MEM_TOKENS = 256
MEM_HEADS = 4
MEM_HEAD_DIM = 256
MEM_W = MEM_HEADS * MEM_HEAD_DIM
MEM_SCALE = MEM_HEAD_DIM ** -0.5
N_BRANCH = 3
BRANCH_W = 1024
PEER_HEADS = 8
PEER_NKEYS = 128
PEER_EXPERTS = PEER_NKEYS * PEER_NKEYS
PEER_DK = 256
PEER_TOPK = 16
PEER_CHUNK = 128
EPS = 1e-6

IN_SPLITS = (ATTN_W, KV_W, KV_W, IDX_HEADS * IDX_DIM, IDX_DIM, IDX_HEADS, LRU_W, LRU_W, MEM_W)
N_IN = sum(IN_SPLITS)

kernel_name = 'dsa_rglru_peer_hybrid_step'


def rmsnorm(x, g):
    x32 = x.astype(jnp.float32)
    y = x32 * lax.rsqrt(jnp.mean(x32 * x32, axis=-1, keepdims=True) + EPS)
    return (y * g.astype(jnp.float32)).astype(x.dtype)


def in_proj(xn, w_in):
    b, t, _ = xn.shape
    proj = xn @ w_in
    offs = np.cumsum((0,) + IN_SPLITS)
    q, k, v, iq, ik, iw, xb, yb, qm = [proj[..., int(offs[i]):int(offs[i + 1])] for i in range(len(IN_SPLITS))]
    return (q.reshape(b, t, N_HEADS, HEAD_DIM), k.reshape(b, t, N_KV_HEADS, HEAD_DIM),
            v.reshape(b, t, N_KV_HEADS, HEAD_DIM), iq.reshape(b, t, IDX_HEADS, IDX_DIM), ik,
            iw * (IDX_HEADS ** -0.5), xb, yb, qm)


def gather_rows(table, idx):
    return jax.vmap(lambda t, i: t[i])(table, idx)


def dsa_attend(q, iq, iw, qpos, ik_all, gather_kv, n_sel):
    b, nq = q.shape[:2]
    n_keys = ik_all.shape[1]
    dots = jnp.einsum('bqhd,bsd->bqsh', iq, ik_all) * IDX_SCALE
    score = jnp.einsum('bqsh,bqh->bqs', jax.nn.relu(dots), iw).astype(jnp.float32)
    causal = jnp.arange(n_keys, dtype=jnp.int32)[None, :] <= qpos[:, None]
    score = jnp.where(causal[None], score, -jnp.inf)
    sel_score, sel = lax.top_k(score, n_sel)
    valid = jnp.isfinite(sel_score)
    kg, vg = gather_kv(sel)
    qg = q.reshape(b, nq, N_KV_HEADS, N_HEADS // N_KV_HEADS, HEAD_DIM)
    logits = jnp.einsum('bqngd,bqknd->bqngk', qg, kg).astype(jnp.float32) * ATTN_SCALE
    logits = jnp.where(valid[:, :, None, None, :], logits, -jnp.inf)
    p = jax.nn.softmax(logits, axis=-1).astype(vg.dtype)
    o = jnp.einsum('bqngk,bqknd->bqngd', p, vg)
    return o.reshape(b, nq, ATTN_W)


def rg_lru_branch(xb, yb, conv_state, h0, conv_w, conv_b, w_rg, b_rg, w_ig, b_ig, lru_lambda):
    b, t, w = xb.shape
    xp = jnp.concatenate([conv_state.astype(xb.dtype), xb], axis=1)
    xc = conv_b + sum(xp[:, j:j + t] * conv_w[j] for j in range(CONV_W))
    new_conv = xp[:, -(CONV_W - 1):]
    x32 = xc.astype(jnp.float32)
    xblk = x32.reshape(b, t, LRU_BLOCKS, LRU_BW)
    r = jax.nn.sigmoid(jnp.einsum('btni,nij->btnj', xblk, w_rg.astype(jnp.float32)).reshape(b, t, w) + b_rg)
    i = jax.nn.sigmoid(jnp.einsum('btni,nij->btnj', xblk, w_ig.astype(jnp.float32)).reshape(b, t, w) + b_ig)
    log_a = -LRU_C * r * jax.nn.softplus(-lru_lambda.astype(jnp.float32))
    a = jnp.exp(log_a)
    u = jnp.sqrt(-jnp.expm1(2.0 * log_a)) * (i * x32)
    u = u.at[:, 0].add(a[:, 0] * h0.astype(jnp.float32))

    def combine(lhs, rhs):
        a1, b1 = lhs
        a2, b2 = rhs
        return a1 * a2, a2 * b1 + b2

    _, h = lax.associative_scan(combine, (a, u), axis=1)
    out = h * jax.nn.gelu(yb.astype(jnp.float32))
    return out.astype(xb.dtype), new_conv, h[:, -1]


def mem_kv(mem, g_mem, w_mem_kv):
    b, m, _ = mem.shape
    kv = rmsnorm(mem, g_mem) @ w_mem_kv
    return (kv[..., :MEM_W].reshape(b, m, MEM_HEADS, MEM_HEAD_DIM),
            kv[..., MEM_W:].reshape(b, m, MEM_HEADS, MEM_HEAD_DIM))


def mem_attend(qm, mk, mv):
    b, t, _ = qm.shape
    q = qm.reshape(b, t, MEM_HEADS, MEM_HEAD_DIM)
    logits = jnp.einsum('bthd,bmhd->bhtm', q, mk).astype(jnp.float32) * MEM_SCALE
    p = jax.nn.softmax(logits, axis=-1).astype(mv.dtype)
    return jnp.einsum('bhtm,bmhd->bthd', p, mv).reshape(b, t, MEM_W)


def merge_branches(xn, o_attn, o_lru, o_mem, w_gate, w_br, w_o):
    b, t, d = xn.shape
    gates = jax.nn.sigmoid((xn @ w_gate).astype(jnp.float32)).reshape(b, t, N_BRANCH, d)
    up = jnp.einsum('btnw,nwd->btnd', jnp.stack([o_attn, o_lru, o_mem], axis=2), w_br).astype(jnp.float32)
    return jnp.sum(gates * up, axis=2).astype(xn.dtype) @ w_o


def peer_ffn(x, w_peer_q, peer_sub_keys, peer_u, peer_v):
    shp = x.shape
    xf = x.reshape(-1, shp[-1])
    n = xf.shape[0]
    xf = jnp.pad(xf, ((0, (-n) % PEER_CHUNK), (0, 0)))
    kk = PEER_TOPK * PEER_TOPK

    def chunk(xc):
        c = xc.shape[0]
        q = (xc @ w_peer_q).reshape(c, PEER_HEADS, 2, PEER_DK // 2)
        s = jnp.einsum('chpd,phkd->chpk', q, peer_sub_keys).astype(jnp.float32)
        s1, i1 = lax.top_k(s[:, :, 0], PEER_TOPK)
        s2, i2 = lax.top_k(s[:, :, 1], PEER_TOPK)
        cand = (s1[..., :, None] + s2[..., None, :]).reshape(c, PEER_HEADS, kk)
        cidx = (i1[..., :, None] * PEER_NKEYS + i2[..., None, :]).reshape(c, PEER_HEADS, kk)
        top_s, pos = lax.top_k(cand, PEER_TOPK)
        e = jnp.take_along_axis(cidx, pos, axis=-1)
        g = jax.nn.softmax(top_s, axis=-1)
        act = jax.nn.gelu(jnp.einsum('chkd,cd->chk', peer_u[e], xc).astype(jnp.float32))
        return jnp.einsum('chk,chkd->cd', (g * act).astype(xc.dtype), peer_v[e])

    out = lax.map(chunk, xf.reshape(-1, PEER_CHUNK, shp[-1]))
    return out.reshape(-1, shp[-1])[:n].reshape(shp)


def residual_tail(x, xn, o_attn, o_lru, o_mem, merge_p, g_ffn, peer_p):
    x = x + merge_branches(xn, o_attn, o_lru, o_mem, *merge_p)
    return x + peer_ffn(rmsnorm(x, g_ffn), *peer_p)


def prompt_mixer(xn, mem, lru_p, w_in, g_mem, w_mem_kv):
    b, s, _ = xn.shape
    q, k, v, iq, ik, iw, xb, yb, qm = in_proj(xn, w_in)
    nb = s // ATTN_BLOCK
    n_sel = min(TOPK_MAX, s // 4)

    def gather_kv(sel):
        return gather_rows(k, sel), gather_rows(v, sel)

    def blocks(a):
        return jnp.swapaxes(a.reshape(b, nb, ATTN_BLOCK, *a.shape[2:]), 0, 1)

    def attend_block(args):
        qb, iqb, iwb, start = args
        qpos = start + jnp.arange(ATTN_BLOCK, dtype=jnp.int32)
        return dsa_attend(qb, iqb, iwb, qpos, ik, gather_kv, n_sel)

    starts = jnp.arange(nb, dtype=jnp.int32) * ATTN_BLOCK
    o_attn = lax.map(attend_block, (blocks(q), blocks(iq), blocks(iw), starts))
    o_attn = jnp.swapaxes(o_attn, 0, 1).reshape(b, s, ATTN_W)
    conv0 = jnp.zeros((b, CONV_W - 1, LRU_W), xb.dtype)
    h0 = jnp.zeros((b, LRU_W), jnp.float32)
    o_lru, conv_new, h_new = rg_lru_branch(xb, yb, conv0, h0, *lru_p)
    mk, mv = mem_kv(mem, g_mem, w_mem_kv)
    o_mem = mem_attend(qm, mk, mv)
    return o_attn, o_lru, o_mem, (k, v, ik, conv_new, h_new.astype(xn.dtype), mk, mv)


def sample_mixer(xn, cache_k, cache_v, cache_idx_k, page_table, state_conv, state_lru, cache_mem_k, cache_mem_v, lru_p, w_in):
    bd, tn, _ = xn.shape
    q, k, v, iq, ik, iw, xb, yb, qm = in_proj(xn, w_in)
    ps = cache_k.shape[1]
    past = page_table.shape[1] * ps
    ik_past = cache_idx_k[page_table].reshape(bd, past, IDX_DIM)
    ik_all = jnp.concatenate([ik_past, ik], axis=1)
    k_pool = cache_k.reshape(-1, N_KV_HEADS, HEAD_DIM)
    v_pool = cache_v.reshape(-1, N_KV_HEADS, HEAD_DIM)

    def gather_kv(sel):
        in_past = (sel < past)[..., None, None]
        sp = jnp.minimum(sel, past - 1)
        page = jnp.take_along_axis(page_table, (sp // ps).reshape(bd, -1), axis=1).reshape(sel.shape)
        phys = page * ps + sp % ps
        sn = jnp.clip(sel - past, 0, tn - 1)
        kg = jnp.where(in_past, k_pool[phys], gather_rows(k, sn))
        vg = jnp.where(in_past, v_pool[phys], gather_rows(v, sn))
        return kg, vg

    qpos = past + jnp.arange(tn, dtype=jnp.int32)
    n_sel = min(TOPK_MAX, (past + tn) // 4)
    o_attn = dsa_attend(q, iq, iw, qpos, ik_all, gather_kv, n_sel)
    o_lru, conv_new, h_new = rg_lru_branch(xb, yb, state_conv, state_lru, *lru_p)
    o_mem = mem_attend(qm, cache_mem_k, cache_mem_v)
    return o_attn, o_lru, o_mem, (k, v, ik, conv_new, h_new.astype(state_lru.dtype))


def setup_inputs(seed: int = 0) -> dict:
    key = jax.random.key(seed)
    ks = iter(jax.random.split(key, 40))

    def nrm(shape, scale):
        return scale * jax.random.normal(next(ks), shape, jnp.float32)

    n_pages = PAST_LEN // PAGE_SIZE
    n_used = DEC_BATCH * n_pages
    n_phys = n_used + n_used // 4
    page_table = jax.random.permutation(next(ks), n_phys)[:n_used].reshape(DEC_BATCH, n_pages).astype(jnp.int32)
    a_c = jax.random.uniform(next(ks), (DEPTH, LRU_W), jnp.float32, 0.9, 0.999)
    sig = a_c ** (1.0 / LRU_C)
    lru_lambda = jnp.log(sig) - jnp.log1p(-sig)
    L = DEPTH
    d = D_MODEL
    return {
        'x_prompt': nrm((BATCH, SEQ, d), 1.0),
        'x_sample': nrm((DEC_BATCH, DEC_SEQ, d), 1.0),
        'cache_k': nrm((L, n_phys, PAGE_SIZE, N_KV_HEADS, HEAD_DIM), 1.0),
        'cache_v': nrm((L, n_phys, PAGE_SIZE, N_KV_HEADS, HEAD_DIM), 1.0),
        'cache_idx_k': nrm((L, n_phys, PAGE_SIZE, IDX_DIM), 1.0),
        'page_table': page_table,
        'state_conv': nrm((L, DEC_BATCH, CONV_W - 1, LRU_W), 1.0),
        'state_lru': nrm((L, DEC_BATCH, LRU_W), 0.5),
        'cache_mem_k': nrm((L, DEC_BATCH, MEM_TOKENS, MEM_HEADS, MEM_HEAD_DIM), 1.0),
        'cache_mem_v': nrm((L, DEC_BATCH, MEM_TOKENS, MEM_HEADS, MEM_HEAD_DIM), 1.0),
        'mem_prompt': nrm((BATCH, MEM_TOKENS, d), 1.0),
        'g_mix': 1.0 + nrm((L, d), 0.02),
        'w_in': nrm((L, d, N_IN), d ** -0.5),
        'conv_w': nrm((L, CONV_W, LRU_W), CONV_W ** -0.5),
        'conv_b': nrm((L, LRU_W), 0.01),
        'w_rg': nrm((L, LRU_BLOCKS, LRU_BW, LRU_BW), LRU_BW ** -0.5),
        'b_rg': nrm((L, LRU_W), 0.01),
        'w_ig': nrm((L, LRU_BLOCKS, LRU_BW, LRU_BW), LRU_BW ** -0.5),
        'b_ig': nrm((L, LRU_W), 0.01),
        'lru_lambda': lru_lambda,
        'g_mem': 1.0 + nrm((L, d), 0.02),
        'w_mem_kv': nrm((L, d, 2 * MEM_W), d ** -0.5),
        'w_gate': nrm((L, d, N_BRANCH * d), d ** -0.5),
        'w_br': nrm((L, N_BRANCH, BRANCH_W, d), BRANCH_W ** -0.5),
        'w_o': nrm((L, d, d), d ** -0.5),
        'g_ffn': 1.0 + nrm((L, d), 0.02),
        'w_peer_q': nrm((L, d, PEER_HEADS * PEER_DK), d ** -0.5),
        'peer_sub_keys': nrm((L, 2, PEER_HEADS, PEER_NKEYS, PEER_DK // 2), (PEER_DK // 2) ** -0.5),
        'peer_u': nrm((L, PEER_EXPERTS, d), d ** -0.5),
        'peer_v': nrm((L, PEER_EXPERTS, d), PEER_HEADS ** -0.5),
        'g_final': 1.0 + nrm((d,), 0.02),
    }


def reference(x_prompt, x_sample, cache_k, cache_v, cache_idx_k, page_table, state_conv, state_lru,
              cache_mem_k, cache_mem_v, mem_prompt, g_mix, w_in, conv_w, conv_b, w_rg, b_rg, w_ig, b_ig,
              lru_lambda, g_mem, w_mem_kv, w_gate, w_br, w_o, g_ffn, w_peer_q, peer_sub_keys, peer_u,
              peer_v, g_final):
    x_p, x_s = x_prompt, x_sample
    st_p, st_s = [], []
    for l in range(DEPTH):
        lru_p = (conv_w[l], conv_b[l], w_rg[l], b_rg[l], w_ig[l], b_ig[l], lru_lambda[l])
        merge_p = (w_gate[l], w_br[l], w_o[l])
        peer_p = (w_peer_q[l], peer_sub_keys[l], peer_u[l], peer_v[l])
        xn = rmsnorm(x_p, g_mix[l])
        oa, ol, om, sp = prompt_mixer(xn, mem_prompt, lru_p, w_in[l], g_mem[l], w_mem_kv[l])
        x_p = residual_tail(x_p, xn, oa, ol, om, merge_p, g_ffn[l], peer_p)
        xn = rmsnorm(x_s, g_mix[l])
        oa, ol, om, ss = sample_mixer(xn, cache_k[l], cache_v[l], cache_idx_k[l], page_table, state_conv[l],
                                      state_lru[l], cache_mem_k[l], cache_mem_v[l], lru_p, w_in[l])
        x_s = residual_tail(x_s, xn, oa, ol, om, merge_p, g_ffn[l], peer_p)
        st_p.append(sp)
        st_s.append(ss)
    y_prompt = rmsnorm(x_p, g_final)
    y_sample = rmsnorm(x_s, g_final)
    k_p, v_p, ik_p, conv_p, h_p, mk_p, mv_p = [jnp.stack(c) for c in zip(*st_p)]
    k_s, v_s, ik_s, conv_s, h_s = [jnp.stack(c) for c in zip(*st_s)]
    return (y_prompt, y_sample, k_p, v_p, ik_p, conv_p, h_p, mk_p, mv_p, k_s, v_s, ik_s, conv_s, h_s)
```

```python
import functools

import jax
import jax.numpy as jnp
from jax import lax
from jax.experimental import pallas as pl
from jax.experimental.pallas import tpu as pltpu

F32 = jnp.float32
BF16 = jnp.bfloat16
I32 = jnp.int32

EPS = 1e-6
NEG_INF = float("-inf")
INT_MIN = -(2 ** 31)

N_HEADS = 8
N_KV_HEADS = 2
HEAD_DIM = 128
IDX_HEADS = 16
IDX_DIM = 64
TOPK_MAX = 256
LRU_BLOCKS = 8
CONV_W = 4
LRU_C = 8.0
MEM_HEADS = 4
N_BRANCH = 3
PEER_HEADS = 8
PEER_NKEYS = 128
PEER_TOPK = 16

LANES = 128
SUBLANES = 8
VMEM_LIMIT = 56 * 1024 * 1024

COL_Q, COL_IQ, COL_XB, COL_YB, COL_QM, COL_K, COL_V, COL_IK = 0, 1024, 2048, 3072, 4096, 5120, 5376, 5632
COL_IW = COL_IK + IDX_DIM
N_PROJ = 6144


def _params(n_axes):
    return pltpu.CompilerParams(dimension_semantics=("arbitrary",) * n_axes, vmem_limit_bytes=VMEM_LIMIT)


def _dot_nt(a, b):
    return lax.dot_general(a, b, (((1,), (1,)), ((), ())), preferred_element_type=F32)


def _dot(a, b):
    return jnp.dot(a, b, preferred_element_type=F32)


def _split3(x):
    hi = x.astype(BF16)
    r1 = x - hi.astype(F32)
    mid = r1.astype(BF16)
    lo = (r1 - mid.astype(F32)).astype(BF16)
    return hi, mid, lo


def _norm_matmul_body(x_ref, g_ref, w_ref, y_ref, *rest, emit_xn):
    if emit_xn:
        xn_ref, xn_scr = rest
    else:
        (xn_scr,) = rest

    @pl.when(pl.program_id(1) == 0)
    def _():
        x = x_ref[...]
        xn = x * lax.rsqrt(jnp.mean(x * x, axis=-1, keepdims=True) + EPS) * g_ref[...]
        xn_scr[...] = xn.astype(BF16)
        if emit_xn:
            xn_ref[...] = xn.astype(xn_ref.dtype)

    y_ref[...] = _dot(xn_scr[...], w_ref[...])


def norm_matmul(x, g, w, *, tm, tn, xn_dtype=None):
    m, d = x.shape
    n = w.shape[1]
    out_shape = [jax.ShapeDtypeStruct((m, n), F32)]
    out_specs = [pl.BlockSpec((tm, tn), lambda i, j: (i, j))]
    if xn_dtype is not None:
        out_shape.append(jax.ShapeDtypeStruct((m, d), xn_dtype))
        out_specs.append(pl.BlockSpec((tm, d), lambda i, j: (i, 0)))
    res = pl.pallas_call(
        functools.partial(_norm_matmul_body, emit_xn=xn_dtype is not None),
        grid=(m // tm, n // tn),
        in_specs=[pl.BlockSpec((tm, d), lambda i, j: (i, 0)),
                  pl.BlockSpec((1, d), lambda i, j: (0, 0)),
                  pl.BlockSpec((d, tn), lambda i, j: (0, j))],
        out_specs=out_specs,
        out_shape=out_shape,
        scratch_shapes=[pltpu.VMEM((tm, d), BF16)],
        compiler_params=_params(2),
        name="norm_matmul",
    )(x, g.reshape(1, d), w)
    return res if xn_dtype is not None else res[0]


def _sortable_key(score):
    bits = pltpu.bitcast(score, I32)
    return jnp.where(bits < 0, bits ^ jnp.int32(0x7FFFFFFF), bits)


def _count(m):
    return jnp.sum(jnp.where(m, 1.0, 0.0), axis=-1, keepdims=True)


def _write_select_mask(key_ref, valid, kpos, n_sel, pos_bits, mask_ref):
    kf = float(n_sel)
    cur0 = jnp.where(_count(key_ref[...] >= 0) >= kf, jnp.int32(0), jnp.int32(INT_MIN))

    def bit_step(i, cur):
        trial = cur + lax.shift_left(jnp.int32(1), jnp.int32(30) - i)
        return jnp.where(_count(key_ref[...] >= trial) >= kf, trial, cur)

    thr = lax.fori_loop(0, 31, bit_step, cur0)
    key = key_ref[...]
    ge = key >= thr
    take_all = thr == INT_MIN
    tied_rows = jnp.logical_and(_count(ge) > kf, jnp.logical_not(take_all))
    mask_ref[...] = jnp.where(jnp.logical_and(ge, valid), 1.0, 0.0)

    @pl.when(jnp.max(jnp.where(tied_rows, 1.0, 0.0)) > 0.0)
    def _():
        gt = key > thr
        tie = key == thr
        want = kf - _count(gt)

        def pos_step(i, last):
            trial = last + lax.shift_left(jnp.int32(1), jnp.int32(pos_bits - 1) - i)
            below = _count(jnp.logical_and(tie, kpos < trial))
            return jnp.where(below < want, trial, last)

        last = lax.fori_loop(0, pos_bits, pos_step, jnp.zeros_like(thr))
        picked = jnp.logical_or(gt, jnp.logical_and(tie, kpos <= last))
        picked = jnp.logical_and(valid, jnp.logical_or(picked, take_all))
        mask_ref[...] = jnp.where(picked, 1.0, 0.0)


def _dsa_prompt_body(q_ref, iq_ref, ikwq_ref, k_ref, v_ref, ikw_ref, o_ref,
                     kbf, vbf, ik_lo, ik_hi, key_scr, mask_scr, *, n_sel, pos_bits):
    qb = pl.program_id(1)
    blk = q_ref.shape[0]
    s = k_ref.shape[0]

    @pl.when(qb == 0)
    def _():
        kbf[...] = k_ref[...].astype(BF16)
        vbf[...] = v_ref[...].astype(BF16)
        ikw = ikw_ref[...]
        lane = lax.broadcasted_iota(I32, ikw.shape, 1)
        lo = jnp.where(lane < IDX_DIM, ikw, 0.0)
        ik_lo[...] = lo.astype(BF16)
        ik_hi[...] = pltpu.roll(lo, IDX_DIM, 1).astype(BF16)

    ikwq = ikwq_ref[...]
    score = jnp.zeros((blk, s), F32)
    for j in range(IDX_HEADS // 2):
        slab = iq_ref[:, j * LANES:(j + 1) * LANES].astype(BF16)
        for half, ik in ((0, ik_lo), (1, ik_hi)):
            h = 2 * j + half
            d = _dot_nt(slab, ik[...])
            w = ikwq[:, IDX_DIM + h:IDX_DIM + h + 1] * (IDX_HEADS ** -0.5)
            score = score + jnp.maximum(d * (IDX_DIM ** -0.5), 0.0) * w

    kpos = lax.broadcasted_iota(I32, (blk, s), 1)
    qpos = qb * blk + lax.broadcasted_iota(I32, (blk, s), 0)
    valid = kpos <= qpos
    key_scr[...] = jnp.where(valid, _sortable_key(score), jnp.int32(INT_MIN))
    _write_select_mask(key_scr, valid, kpos, n_sel, pos_bits, mask_scr)

    sel = mask_scr[...] > 0.5
    group = N_HEADS // N_KV_HEADS
    for h in range(N_HEADS):
        n = h // group
        qh = q_ref[:, h * HEAD_DIM:(h + 1) * HEAD_DIM].astype(BF16)
        logits = _dot_nt(qh, kbf[:, n * HEAD_DIM:(n + 1) * HEAD_DIM]) * (HEAD_DIM ** -0.5)
        logits = jnp.where(sel, logits, NEG_INF)
        m = jnp.max(logits, axis=-1, keepdims=True)
        p = jnp.exp(logits - m)
        l = jnp.sum(p, axis=-1, keepdims=True)
        o = _dot(p.astype(BF16), vbf[:, n * HEAD_DIM:(n + 1) * HEAD_DIM]) / l
        o_ref[:, h * HEAD_DIM:(h + 1) * HEAD_DIM] = o.astype(o_ref.dtype)


def dsa_prompt(proj, batch, seq, *, blk=128):
    nq = seq // blk
    n_sel = min(TOPK_MAX, seq // 4)
    pos_bits = max(1, (seq - 1).bit_length())
    kv_w = N_KV_HEADS * HEAD_DIM
    attn_w = N_HEADS * HEAD_DIM
    return pl.pallas_call(
        functools.partial(_dsa_prompt_body, n_sel=n_sel, pos_bits=pos_bits),
        grid=(batch, nq),
        in_specs=[pl.BlockSpec((blk, attn_w), lambda b, i: (b * nq + i, COL_Q // attn_w)),
                  pl.BlockSpec((blk, attn_w), lambda b, i: (b * nq + i, COL_IQ // attn_w)),
                  pl.BlockSpec((blk, LANES), lambda b, i: (b * nq + i, COL_IK // LANES)),
                  pl.BlockSpec((seq, kv_w), lambda b, i: (b, COL_K // kv_w)),
                  pl.BlockSpec((seq, kv_w), lambda b, i: (b, COL_V // kv_w)),
                  pl.BlockSpec((seq, LANES), lambda b, i: (b, COL_IK // LANES))],
        out_specs=pl.BlockSpec((blk, attn_w), lambda b, i: (b * nq + i, 0)),
        out_shape=jax.ShapeDtypeStruct((batch * seq, attn_w), BF16),
        scratch_shapes=[pltpu.VMEM((seq, kv_w), BF16), pltpu.VMEM((seq, kv_w), BF16),
                        pltpu.VMEM((seq, LANES), BF16), pltpu.VMEM((seq, LANES), BF16),
                        pltpu.VMEM((blk, seq), I32), pltpu.VMEM((blk, seq), F32)],
        compiler_params=_params(2),
        name="dsa_prompt",
    )(proj, proj, proj, proj, proj, proj)


def _dsa_sample_select_body(pt_ref, iq_ref, iw_ref, ikp_ref, ikn_ref, mask_ref, score_scr, key_scr,
                            *, n_sel, pos_bits, tn):
    del pt_ref
    p = pl.program_id(1)
    n_pages = pl.num_programs(1)
    page = ikp_ref.shape[0]
    rows = mask_ref.shape[0]
    past = n_pages * page
    width = score_scr.shape[1]

    iq = iq_ref[...].astype(BF16)
    iw = iw_ref[...] * (IDX_HEADS ** -0.5)
    r_i = lax.broadcasted_iota(I32, (rows, iq.shape[0]), 0)
    c_i = lax.broadcasted_iota(I32, (rows, iq.shape[0]), 1)
    gsum = jnp.where((c_i // IDX_HEADS) == (r_i % tn), 1.0, 0.0).astype(BF16)

    def page_scores(ik):
        d = _dot_nt(iq, ik.astype(BF16))
        xw = jnp.maximum(d * (IDX_DIM ** -0.5), 0.0) * iw
        hi, mid, lo = _split3(xw)
        return _dot(gsum, hi) + _dot(gsum, mid) + _dot(gsum, lo)

    @pl.when(p == 0)
    def _():
        score_scr[:, past:] = jnp.zeros((rows, width - past), F32)

    score_scr[:, pl.ds(pl.multiple_of(p * page, page), page)] = page_scores(ikp_ref[...])

    @pl.when(p == n_pages - 1)
    def _():
        score_scr[:, past:past + page] = page_scores(ikn_ref[...])
        kpos = lax.broadcasted_iota(I32, (rows, width), 1)
        qpos = past + lax.broadcasted_iota(I32, (rows, width), 0) % tn
        valid = kpos <= qpos
        key_scr[...] = jnp.where(valid, _sortable_key(score_scr[...]), jnp.int32(INT_MIN))
        _write_select_mask(key_scr, valid, kpos, n_sel, pos_bits, mask_ref)


def _dsa_sample_attend_body(pt_ref, q_ref, kp_ref, vp_ref, kn_ref, vn_ref, mask_ref, o_ref,
                            m_scr, l_scr, acc_scr):
    del pt_ref
    p = pl.program_id(1)
    n_pages = pl.num_programs(1)
    page = kp_ref.shape[0]
    past = n_pages * page
    rows = mask_ref.shape[0]

    @pl.when(p == 0)
    def _():
        m_scr[...] = jnp.full(m_scr.shape, NEG_INF, F32)
        l_scr[...] = jnp.zeros(l_scr.shape, F32)
        acc_scr[...] = jnp.zeros(acc_scr.shape, F32)

    def update(k, v, sel):
        for n in range(N_KV_HEADS):
            rs = slice(n * rows, (n + 1) * rows)
            cs = slice(n * HEAD_DIM, (n + 1) * HEAD_DIM)
            logits = _dot_nt(q_ref[rs, :].astype(BF16), k[:, cs].astype(BF16)) * (HEAD_DIM ** -0.5)
            logits = jnp.where(sel, logits, NEG_INF)
            m_old = m_scr[rs, :]
            m_new = jnp.maximum(m_old, jnp.max(logits, axis=-1, keepdims=True))
            m_safe = jnp.where(m_new == NEG_INF, 0.0, m_new)
            alpha = jnp.exp(m_old - m_safe)
            pr = jnp.exp(logits - m_safe)
            l_scr[rs, :] = alpha * l_scr[rs, :] + jnp.sum(pr, axis=-1, keepdims=True)
            acc_scr[rs, :] = alpha * acc_scr[rs, :] + _dot(pr.astype(BF16), v[:, cs].astype(BF16))
            m_scr[rs, :] = m_new

    sel = mask_ref[:, pl.ds(pl.multiple_of(p * page, page), page)] > 0.5
    update(kp_ref[...], vp_ref[...], sel)

    @pl.when(p == n_pages - 1)
    def _():
        update(kn_ref[...], vn_ref[...], mask_ref[:, past:past + page] > 0.5)
        o_ref[...] = (acc_scr[...] / l_scr[...]).astype(o_ref.dtype)


def dsa_sample(proj_s, cache_k, cache_v, cache_idx_k, page_table, tn):
    bd, n_pages = page_table.shape
    n_phys, page = cache_k.shape[0], cache_k.shape[1]
    past = n_pages * page
    group = N_HEADS // N_KV_HEADS
    rows = group * tn
    kv_w = N_KV_HEADS * HEAD_DIM
    width = past + page
    n_sel = min(TOPK_MAX, (past + tn) // 4)
    pos_bits = max(1, (width - 1).bit_length())

    iq = proj_s[:, COL_IQ:COL_IQ + IDX_HEADS * IDX_DIM].reshape(bd, tn * IDX_HEADS, IDX_DIM)
    iq = jnp.pad(iq, ((0, 0), (0, LANES - tn * IDX_HEADS), (0, 0)))
    iw = proj_s[:, COL_IW:COL_IW + IDX_HEADS].reshape(bd, tn * IDX_HEADS, 1)
    iw = jnp.pad(iw, ((0, 0), (0, LANES - tn * IDX_HEADS), (0, 0)))
    ik_new = jnp.pad(proj_s[:, COL_IK:COL_IK + IDX_DIM].reshape(bd, tn, IDX_DIM), ((0, 0), (0, page - tn), (0, 0)))
    k_new = jnp.pad(proj_s[:, COL_K:COL_K + kv_w].reshape(bd, tn, kv_w), ((0, 0), (0, page - tn), (0, 0)))
    v_new = jnp.pad(proj_s[:, COL_V:COL_V + kv_w].reshape(bd, tn, kv_w), ((0, 0), (0, page - tn), (0, 0)))
    q = proj_s[:, COL_Q:COL_Q + N_HEADS * HEAD_DIM].reshape(bd, tn, N_HEADS, HEAD_DIM)
    q = q.transpose(0, 2, 1, 3).reshape(bd, N_HEADS * tn, HEAD_DIM)

    mask = pl.pallas_call(
        functools.partial(_dsa_sample_select_body, n_sel=n_sel, pos_bits=pos_bits, tn=tn),
        grid_spec=pltpu.PrefetchScalarGridSpec(
            num_scalar_prefetch=1,
            grid=(bd, n_pages),
            in_specs=[pl.BlockSpec((None, LANES, IDX_DIM), lambda b, p, pt: (b, 0, 0)),
                      pl.BlockSpec((None, LANES, 1), lambda b, p, pt: (b, 0, 0)),
                      pl.BlockSpec((None, page, IDX_DIM), lambda b, p, pt: (pt[b, p], 0, 0)),
                      pl.BlockSpec((None, page, IDX_DIM), lambda b, p, pt: (b, 0, 0))],
            out_specs=pl.BlockSpec((None, rows, width), lambda b, p, pt: (b, 0, 0)),
            scratch_shapes=[pltpu.VMEM((rows, width), F32), pltpu.VMEM((rows, width), I32)]),
        out_shape=jax.ShapeDtypeStruct((bd, rows, width), F32),
        compiler_params=_params(2),
        name="dsa_sample_select",
    )(page_table, iq, iw, cache_idx_k, ik_new)

    o = pl.pallas_call(
        _dsa_sample_attend_body,
        grid_spec=pltpu.PrefetchScalarGridSpec(
            num_scalar_prefetch=1,
            grid=(bd, n_pages),
            in_specs=[pl.BlockSpec((None, N_HEADS * tn, HEAD_DIM), lambda b, p, pt: (b, 0, 0)),
                      pl.BlockSpec((None, page, kv_w), lambda b, p, pt: (pt[b, p], 0, 0)),
                      pl.BlockSpec((None, page, kv_w), lambda b, p, pt: (pt[b, p], 0, 0)),
                      pl.BlockSpec((None, page, kv_w), lambda b, p, pt: (b, 0, 0)),
                      pl.BlockSpec((None, page, kv_w), lambda b, p, pt: (b, 0, 0)),
                      pl.BlockSpec((None, rows, width), lambda b, p, pt: (b, 0, 0))],
            out_specs=pl.BlockSpec((None, N_HEADS * tn, HEAD_DIM), lambda b, p, pt: (b, 0, 0)),
            scratch_shapes=[pltpu.VMEM((N_HEADS * tn, 1), F32), pltpu.VMEM((N_HEADS * tn, 1), F32),
                            pltpu.VMEM((N_HEADS * tn, HEAD_DIM), F32)]),
        out_shape=jax.ShapeDtypeStruct((bd, N_HEADS * tn, HEAD_DIM), BF16),
        compiler_params=_params(2),
        name="dsa_sample_attend",
    )(page_table, q, cache_k.reshape(n_phys, page, kv_w), cache_v.reshape(n_phys, page, kv_w), k_new, v_new, mask)
    return o.reshape(bd, N_HEADS, tn, HEAD_DIM).transpose(0, 2, 1, 3).reshape(bd * tn, N_HEADS * HEAD_DIM)


def _softplus(z):
    return jnp.maximum(z, 0.0) + jnp.log1p(jnp.exp(-jnp.abs(z)))


def _rglru_body(xb_ref, yb_ref, cs_ref, h0_ref, cw_ref, cb_ref, wrg_ref, brg_ref, wig_ref, big_ref, lam_ref,
                o_ref, cnew_ref, hlast_ref, xp_scr, h_scr):
    c = pl.program_id(1)
    tt = xb_ref.shape[0]
    w = xb_ref.shape[1]
    bw = w // LRU_BLOCKS
    hist = CONV_W - 1
    base = SUBLANES

    @pl.when(c == 0)
    def _():
        xp_scr[base - hist:base, :] = cs_ref[...]
        h_scr[...] = h0_ref[...]

    xp_scr[base:base + tt, :] = xb_ref[...]
    xc = cb_ref[...] + sum(xp_scr[base - hist + j:base - hist + j + tt, :] * cw_ref[j:j + 1, :] for j in range(CONV_W))
    new_conv = xp_scr[base + tt - hist:base + tt, :]
    cnew_ref[...] = new_conv
    xp_scr[base - hist:base, :] = new_conv

    neg_sp = -LRU_C * _softplus(-lam_ref[...])
    row = lax.broadcasted_iota(I32, (tt, bw), 0)
    for n in range(LRU_BLOCKS):
        cs = slice(n * bw, (n + 1) * bw)
        x32 = xc[:, cs]
        xbf = x32.astype(BF16)
        r = jax.nn.sigmoid(_dot(xbf, wrg_ref[n]) + brg_ref[:, cs])
        i = jax.nn.sigmoid(_dot(xbf, wig_ref[n]) + big_ref[:, cs])
        log_a = r * neg_sp[:, cs]
        a = jnp.exp(log_a)
        u = jnp.sqrt(-jnp.tanh(log_a) * (a * a + 1.0)) * (i * x32)
        gate = jax.nn.gelu(yb_ref[:, cs])
        h_in = h_scr[:, cs]
        if tt <= SUBLANES:
            h = h_in
            for t in range(tt):
                h = a[t:t + 1, :] * h + u[t:t + 1, :]
                o_ref[t:t + 1, cs] = (h * gate[t:t + 1, :]).astype(o_ref.dtype)
            h_last = h
        else:
            aa, bb = a, u
            d = 1
            while d < tt:
                a_sh = pltpu.roll(aa, d, 0)
                b_sh = pltpu.roll(bb, d, 0)
                ok = row >= d
                bb = jnp.where(ok, aa * b_sh + bb, bb)
                aa = jnp.where(ok, aa * a_sh, aa)
                d *= 2
            h = aa * h_in + bb
            o_ref[:, cs] = (h * gate).astype(o_ref.dtype)
            h_last = h[tt - 1:tt, :]
        h_scr[:, cs] = h_last
        hlast_ref[:, cs] = h_last


def rglru(xb, yb, xb_spec, yb_spec, out_spec, out_rows, conv_state, h0, batch, nc, chunk, lru_p):
    conv_w, conv_b, w_rg, b_rg, w_ig, b_ig, lru_lambda = lru_p
    w = conv_w.shape[1]
    hist = CONV_W - 1
    vec = lambda a: a.reshape(1, w)
    full2 = lambda shape: pl.BlockSpec(shape, lambda b, c: (0,) * len(shape))
    out, cnew, hlast = pl.pallas_call(
        _rglru_body,
        grid=(batch, nc),
        in_specs=[xb_spec, yb_spec,
                  pl.BlockSpec((None, hist, w), lambda b, c: (b, 0, 0)),
                  pl.BlockSpec((None, 1, w), lambda b, c: (b, 0, 0)),
                  full2((CONV_W, w)), full2((1, w)),
                  full2(w_rg.shape), full2((1, w)), full2(w_ig.shape), full2((1, w)), full2((1, w))],
        out_specs=[out_spec,
                   pl.BlockSpec((None, hist, w), lambda b, c: (b, 0, 0)),
                   pl.BlockSpec((None, 1, w), lambda b, c: (b, 0, 0))],
        out_shape=[jax.ShapeDtypeStruct(out_rows + (w,), BF16),
                   jax.ShapeDtypeStruct((batch, hist, w), F32),
                   jax.ShapeDtypeStruct((batch, 1, w), F32)],
        scratch_shapes=[pltpu.VMEM((SUBLANES + chunk, w), F32), pltpu.VMEM((1, w), F32)],
        compiler_params=_params(2),
        name="rglru",
    )(xb, yb, conv_state, h0.reshape(batch, 1, w), conv_w, vec(conv_b), w_rg.astype(BF16), vec(b_rg),
      w_ig.astype(BF16), vec(b_ig), vec(lru_lambda))
    return out, cnew, hlast.reshape(batch, w)


def _mem_attn_body(q_ref, k_ref, v_ref, o_ref):
    hd = k_ref.shape[1] // MEM_HEADS
    for h in range(MEM_HEADS):
        cs = slice(h * hd, (h + 1) * hd)
        logits = _dot_nt(q_ref[:, cs].astype(BF16), k_ref[:, cs].astype(BF16)) * (hd ** -0.5)
        m = jnp.max(logits, axis=-1, keepdims=True)
        p = jnp.exp(logits - m)
        l = jnp.sum(p, axis=-1, keepdims=True)
        o = _dot(p.astype(BF16), v_ref[:, cs].astype(BF16)) / l
        o_ref[:, cs] = o.astype(o_ref.dtype)


def mem_attn_prompt(proj, mkv, batch, seq, mem_tokens, *, tq):
    w = mkv.shape[1] // 2
    nq = seq // tq
    return pl.pallas_call(
        _mem_attn_body,
        grid=(batch, nq),
        in_specs=[pl.BlockSpec((tq, w), lambda b, i: (b * nq + i, COL_QM // w)),
                  pl.BlockSpec((mem_tokens, w), lambda b, i: (b, 0)),
                  pl.BlockSpec((mem_tokens, w), lambda b, i: (b, 1))],
        out_specs=pl.BlockSpec((tq, w), lambda b, i: (b * nq + i, 0)),
        out_shape=jax.ShapeDtypeStruct((batch * seq, w), BF16),
        compiler_params=_params(2),
        name="mem_attn_prompt",
    )(proj, mkv, mkv)


def mem_attn_sample(qm, mem_k, mem_v):
    bd, tn, w = qm.shape
    m = mem_k.shape[1]
    return pl.pallas_call(
        _mem_attn_body,
        grid=(bd,),
        in_specs=[pl.BlockSpec((None, tn, w), lambda b: (b, 0, 0)),
                  pl.BlockSpec((None, m, w), lambda b: (b, 0, 0)),
                  pl.BlockSpec((None, m, w), lambda b: (b, 0, 0))],
        out_specs=pl.BlockSpec((None, tn, w), lambda b: (b, 0, 0)),
        out_shape=jax.ShapeDtypeStruct((bd, tn, w), BF16),
        compiler_params=_params(1),
        name="mem_attn_sample",
    )(qm, mem_k, mem_v)


def _merge_body(xn_ref, oa_ref, ol_ref, om_ref, wg0, wg1, wg2, wb0, wb1, wb2, out_ref):
    xn = xn_ref[...]
    acc = None
    for o_ref, wg, wb in ((oa_ref, wg0, wb0), (ol_ref, wg1, wb1), (om_ref, wg2, wb2)):
        gate = jax.nn.sigmoid(_dot(xn, wg[...]))
        term = gate * _dot(o_ref[...], wb[...])
        acc = term if acc is None else acc + term
    out_ref[...] = acc.astype(out_ref.dtype)


def merge(xn, o_attn, o_lru, o_mem, w_gate, w_br, *, tm, tn):
    m, d = xn.shape
    bw = o_attn.shape[1]
    nj = d // tn
    wg_specs = [pl.BlockSpec((d, tn), lambda i, j, n=n: (0, n * nj + j)) for n in range(N_BRANCH)]
    wb_specs = [pl.BlockSpec((None, bw, tn), lambda i, j, n=n: (n, 0, j)) for n in range(N_BRANCH)]
    o_spec = pl.BlockSpec((tm, bw), lambda i, j: (i, 0))
    return pl.pallas_call(
        _merge_body,
        grid=(m // tm, nj),
        in_specs=[pl.BlockSpec((tm, d), lambda i, j: (i, 0)), o_spec, o_spec, o_spec] + wg_specs + wb_specs,
        out_specs=pl.BlockSpec((tm, tn), lambda i, j: (i, j)),
        out_shape=jax.ShapeDtypeStruct((m, d), BF16),
        compiler_params=_params(2),
        name="merge",
    )(xn, o_attn, o_lru, o_mem, w_gate, w_gate, w_gate, w_br, w_br, w_br)


def _matres_body(a_ref, w_ref, r_ref, y_ref):
    y_ref[...] = r_ref[...] + _dot(a_ref[...], w_ref[...])


def matmul_residual(a, w, r, *, tm, tn):
    m, k = a.shape
    n = w.shape[1]
    return pl.pallas_call(
        _matres_body,
        grid=(m // tm, n // tn),
        in_specs=[pl.BlockSpec((tm, k), lambda i, j: (i, 0)),
                  pl.BlockSpec((k, tn), lambda i, j: (0, j)),
                  pl.BlockSpec((tm, tn), lambda i, j: (i, j))],
        out_specs=pl.BlockSpec((tm, tn), lambda i, j: (i, j)),
        out_shape=jax.ShapeDtypeStruct((m, n), F32),
        compiler_params=_params(2),
        name="matmul_residual",
    )(a, w, r)


def _peer_route_body(q_ref, keys_ref, e_ref, g_ref, s_scr, val_scr, idx_scr, cand_scr, code_scr, top_scr, pick_scr):
    tm = q_ref.shape[0]
    dk = keys_ref.shape[3]
    nk = keys_ref.shape[2]
    n_cand = PEER_TOPK * PEER_TOPK
    code_span = float(nk * nk)

    for h in range(PEER_HEADS):
        for half in range(2):
            hp = 2 * h + half
            qs = q_ref[:, hp * dk:(hp + 1) * dk].astype(BF16)
            s_scr[hp * tm:(hp + 1) * tm, :] = _dot_nt(qs, keys_ref[half, h])

    lane_i = lax.broadcasted_iota(I32, s_scr.shape, 1)
    lane_f = lane_i.astype(F32)
    val_scr[...] = jnp.zeros(val_scr.shape, F32)
    idx_scr[...] = jnp.zeros(idx_scr.shape, F32)

    def stage1(it, carry):
        s = s_scr[...]
        m = jnp.max(s, axis=-1, keepdims=True)
        pos = jnp.min(jnp.where(s == m, lane_f, float(nk)), axis=-1, keepdims=True)
        slot = lane_i == it
        val_scr[...] = jnp.where(slot, m, val_scr[...])
        idx_scr[...] = jnp.where(slot, pos, idx_scr[...])
        s_scr[...] = jnp.where(lane_f == pos, NEG_INF, s)
        return carry

    lax.fori_loop(0, PEER_TOPK, stage1, 0)

    r_i = lax.broadcasted_iota(I32, (nk, n_cand), 0)
    c_i = lax.broadcasted_iota(I32, (nk, n_cand), 1)
    e1 = jnp.where((c_i // PEER_TOPK) == r_i, 1.0, 0.0).astype(BF16)
    e2 = jnp.where((c_i % PEER_TOPK) == r_i, 1.0, 0.0).astype(BF16)
    cpos = lax.broadcasted_iota(I32, (tm, n_cand), 1).astype(F32)
    for h in range(PEER_HEADS):
        r1 = slice((2 * h) * tm, (2 * h + 1) * tm)
        r2 = slice((2 * h + 1) * tm, (2 * h + 2) * tm)
        cand = sum(_dot(t, e1) for t in _split3(val_scr[r1, :])) + sum(_dot(t, e2) for t in _split3(val_scr[r2, :]))
        eid = _dot(idx_scr[r1, :].astype(BF16), e1) * float(nk) + _dot(idx_scr[r2, :].astype(BF16), e2)
        cand_scr[h * tm:(h + 1) * tm, :] = cand
        code_scr[h * tm:(h + 1) * tm, :] = cpos * code_span + eid

    lane2_i = lax.broadcasted_iota(I32, top_scr.shape, 1)
    top_scr[...] = jnp.zeros(top_scr.shape, F32)
    pick_scr[...] = jnp.zeros(pick_scr.shape, F32)
    big = float(n_cand) * code_span

    def stage2(it, carry):
        c = cand_scr[...]
        code = code_scr[...]
        m = jnp.max(c, axis=-1, keepdims=True)
        pc = jnp.min(jnp.where(c == m, code, big), axis=-1, keepdims=True)
        slot = lane2_i == it
        top_scr[...] = jnp.where(slot, m, top_scr[...])
        pick_scr[...] = jnp.where(slot, pc, pick_scr[...])
        cand_scr[...] = jnp.where(code == pc, NEG_INF, c)
        return carry

    lax.fori_loop(0, PEER_TOPK, stage2, 0)

    lane = lax.broadcasted_iota(I32, (tm, LANES), 1)
    e_acc = jnp.zeros((tm, LANES), F32)
    g_acc = jnp.zeros((tm, LANES), F32)
    for h in range(PEER_HEADS):
        rs = slice(h * tm, (h + 1) * tm)
        ts = top_scr[rs, :]
        pk = pick_scr[rs, :]
        eid = pk - code_span * jnp.floor(pk / code_span)
        ex = jnp.where(lane < PEER_TOPK, jnp.exp(ts - ts[:, 0:1]), 0.0)
        g = ex / jnp.sum(ex, axis=-1, keepdims=True)
        eid = jnp.where(lane < PEER_TOPK, eid, 0.0)
        if h:
            g = pltpu.roll(g, h * PEER_TOPK, 1)
            eid = pltpu.roll(eid, h * PEER_TOPK, 1)
        g_acc = g_acc + g
        e_acc = e_acc + eid
    e_ref[...] = e_acc.astype(I32)
    g_ref[...] = g_acc


def peer_route(qp, sub_keys, *, tm):
    t, d = qp.shape
    n_cand = PEER_TOPK * PEER_TOPK
    rows1 = 2 * PEER_HEADS * tm
    rows2 = PEER_HEADS * tm
    return pl.pallas_call(
        _peer_route_body,
        grid=(t // tm,),
        in_specs=[pl.BlockSpec((tm, d), lambda i: (i, 0)),
                  pl.BlockSpec(sub_keys.shape, lambda i: (0, 0, 0, 0))],
        out_specs=[pl.BlockSpec((tm, LANES), lambda i: (i, 0)), pl.BlockSpec((tm, LANES), lambda i: (i, 0))],
        out_shape=[jax.ShapeDtypeStruct((t, LANES), I32), jax.ShapeDtypeStruct((t, LANES), F32)],
        scratch_shapes=[pltpu.VMEM((rows1, PEER_NKEYS), F32), pltpu.VMEM((rows1, PEER_NKEYS), F32),
                        pltpu.VMEM((rows1, PEER_NKEYS), F32),
                        pltpu.VMEM((rows2, n_cand), F32), pltpu.VMEM((rows2, n_cand), F32),
                        pltpu.VMEM((rows2, LANES), F32), pltpu.VMEM((rows2, LANES), F32)],
        compiler_params=_params(1),
        name="peer_route",
    )(qp, sub_keys)


def _peer_experts_body(ids_hbm, g_ref, xn_ref, x1_ref, gfin_ref, u_hbm, v_hbm, y_ref,
                       ids_smem, ubuf, vbuf, out_scr, sem_ids, sem_u, sem_v, *, tb, final_norm):
    i = pl.program_id(0)
    n = pl.num_programs(0)
    pairs = LANES
    npairs = tb * pairs
    slot = i % 2
    nxt = 1 - slot

    def ids_copy(tile, s):
        return pltpu.make_async_copy(ids_hbm.at[pl.ds(tile * npairs, npairs)], ids_smem.at[s], sem_ids.at[s])

    def row_copy(src_hbm, buf, sem, e, s, k):
        return pltpu.make_async_copy(src_hbm.at[pl.ds(e, 1)], buf.at[s, pl.ds(k, 1)], sem.at[s])

    def issue_gather(s):
        def body(k, carry):
            e = ids_smem[s, k]
            row_copy(u_hbm, ubuf, sem_u, e, s, k).start()
            row_copy(v_hbm, vbuf, sem_v, e, s, k).start()
            return carry
        lax.fori_loop(0, npairs, body, 0, unroll=8)

    def wait_gather(s):
        pltpu.make_async_copy(u_hbm.at[pl.ds(0, npairs)], ubuf.at[s], sem_u.at[s]).wait()
        pltpu.make_async_copy(v_hbm.at[pl.ds(0, npairs)], vbuf.at[s], sem_v.at[s]).wait()

    @pl.when(i == 0)
    def _():
        ids_copy(0, 0).start()
        ids_copy(0, 0).wait()
        issue_gather(0)

        @pl.when(n > 1)
        def _():
            ids_copy(1, 1).start()

    @pl.when(i + 1 < n)
    def _():
        ids_copy(i + 1, nxt).wait()
        issue_gather(nxt)

    wait_gather(slot)

    @pl.when(i + 2 < n)
    def _():
        ids_copy(i + 2, slot).start()

    g_t = g_ref[...].T
    for t in range(tb):
        rs = slice(t * pairs, (t + 1) * pairs)
        x_t = xn_ref[t:t + 1, :]
        act = jax.nn.gelu(jnp.sum(ubuf[slot, rs, :] * x_t, axis=-1, keepdims=True))
        wgt = g_t[:, t:t + 1] * act
        out_scr[t:t + 1, :] = jnp.sum(wgt * vbuf[slot, rs, :], axis=0, keepdims=True)

    xf = x1_ref[...] + out_scr[...]
    if final_norm:
        xf = xf * lax.rsqrt(jnp.mean(xf * xf, axis=-1, keepdims=True) + EPS) * gfin_ref[...]
    y_ref[...] = xf


def peer_experts(ids, gates, xn, x1, g_final, peer_u, peer_v, *, final_norm, tb=8):
    t, d = xn.shape
    npairs = tb * LANES
    return pl.pallas_call(
        functools.partial(_peer_experts_body, tb=tb, final_norm=final_norm),
        grid=(t // tb,),
        in_specs=[pl.BlockSpec(memory_space=pl.ANY),
                  pl.BlockSpec((tb, LANES), lambda i: (i, 0)),
                  pl.BlockSpec((tb, d), lambda i: (i, 0)),
                  pl.BlockSpec((tb, d), lambda i: (i, 0)),
                  pl.BlockSpec((1, d), lambda i: (0, 0)),
                  pl.BlockSpec(memory_space=pl.ANY),
                  pl.BlockSpec(memory_space=pl.ANY)],
        out_specs=pl.BlockSpec((tb, d), lambda i: (i, 0)),
        out_shape=jax.ShapeDtypeStruct((t, d), F32),
        scratch_shapes=[pltpu.SMEM((2, npairs), I32),
                        pltpu.VMEM((2, npairs, d), F32), pltpu.VMEM((2, npairs, d), F32),
                        pltpu.VMEM((tb, d), F32),
                        pltpu.SemaphoreType.DMA((2,)), pltpu.SemaphoreType.DMA((2,)), pltpu.SemaphoreType.DMA((2,))],
        compiler_params=_params(1),
        name="peer_experts",
    )(ids.reshape(t * LANES), gates, xn, x1, g_final.reshape(1, d), peer_u, peer_v)


def _pad_rows(a, mult):
    pad = (-a.shape[0]) % mult
    return jnp.pad(a, ((0, pad),) + ((0, 0),) * (a.ndim - 1)) if pad else a


def _pack_w_in(w_in):
    attn_w = N_HEADS * HEAD_DIM
    kv_w = N_KV_HEADS * HEAD_DIM
    iq_w = IDX_HEADS * IDX_DIM
    d = w_in.shape[0]
    branch_w = (w_in.shape[1] - attn_w - 2 * kv_w - iq_w - IDX_DIM - IDX_HEADS) // 3
    sizes = (attn_w, kv_w, kv_w, iq_w, IDX_DIM, IDX_HEADS, branch_w, branch_w, branch_w)
    offs = [0]
    for s in sizes:
        offs.append(offs[-1] + s)
    q, k, v, iq, ik, iw, xb, yb, qm = [w_in[:, offs[i]:offs[i + 1]] for i in range(len(sizes))]
    pad = jnp.zeros((d, N_PROJ - (COL_IW + IDX_HEADS)), w_in.dtype)
    return jnp.concatenate([q, iq, xb, yb, qm, k, v, ik, iw, pad], axis=1).astype(BF16)


def kernel(x_prompt, x_sample, cache_k, cache_v, cache_idx_k, page_table, state_conv, state_lru, cache_mem_k,
           cache_mem_v, mem_prompt, g_mix, w_in, conv_w, conv_b, w_rg, b_rg, w_ig, b_ig, lru_lambda, g_mem,
           w_mem_kv, w_gate, w_br, w_o, g_ffn, w_peer_q, peer_sub_keys, peer_u, peer_v, g_final):
    depth = w_in.shape[0]
    b, s, d = x_prompt.shape
    bd, tn, _ = x_sample.shape
    tp, ts = b * s, bd * tn
    mem_tokens = mem_prompt.shape[1]
    kv_w = N_KV_HEADS * HEAD_DIM
    lru_w = conv_w.shape[2]
    tm = 512
    x_all = _pad_rows(jnp.concatenate([x_prompt.reshape(tp, d), x_sample.reshape(ts, d)]), tm)
    st_p, st_s = [], []
    for l in range(depth):
        lru_p = (conv_w[l], conv_b[l], w_rg[l], b_rg[l], w_ig[l], b_ig[l], lru_lambda[l])
        proj, xn = norm_matmul(x_all, g_mix[l], _pack_w_in(w_in[l]), tm=tm, tn=1024, xn_dtype=BF16)
        proj_s = proj[tp:tp + ts]

        oa_p = dsa_prompt(proj, b, s)
        chunk = min(s, 256)
        nc = s // chunk
        ol_p, conv_p, h_p = rglru(
            proj, proj,
            pl.BlockSpec((chunk, lru_w), lambda bi, c: (bi * nc + c, COL_XB // lru_w)),
            pl.BlockSpec((chunk, lru_w), lambda bi, c: (bi * nc + c, COL_YB // lru_w)),
            pl.BlockSpec((chunk, lru_w), lambda bi, c: (bi * nc + c, 0)), (tp,),
            jnp.zeros((b, CONV_W - 1, lru_w), F32), jnp.zeros((b, lru_w), F32), b, nc, chunk, lru_p)
        mem_rows = _pad_rows(mem_prompt.reshape(b * mem_tokens, d), tm)
        mkv = norm_matmul(mem_rows, g_mem[l], w_mem_kv[l].astype(BF16), tm=tm, tn=1024)
        om_p = mem_attn_prompt(proj, mkv, b, s, mem_tokens, tq=min(s, 512))
        mem_w = mkv.shape[1] // 2
        mk_p = mkv[:b * mem_tokens, :mem_w].reshape(b, mem_tokens, MEM_HEADS, mem_w // MEM_HEADS)
        mv_p = mkv[:b * mem_tokens, mem_w:].reshape(b, mem_tokens, MEM_HEADS, mem_w // MEM_HEADS)

        oa_s = dsa_sample(proj_s, cache_k[l], cache_v[l], cache_idx_k[l], page_table, tn)
        xb_s = proj_s[:, COL_XB:COL_XB + lru_w].reshape(bd, tn, lru_w)
        yb_s = proj_s[:, COL_YB:COL_YB + lru_w].reshape(bd, tn, lru_w)
        step_spec = pl.BlockSpec((None, tn, lru_w), lambda bi, c: (bi, 0, 0))
        ol_s, conv_s, h_s = rglru(xb_s, yb_s, step_spec, step_spec, step_spec, (bd, tn),
                                  state_conv[l], state_lru[l], bd, 1, tn, lru_p)
        ol_s = ol_s.reshape(ts, lru_w)
        qm_s = proj_s[:, COL_QM:COL_QM + mem_w].reshape(bd, tn, mem_w)
        om_s = mem_attn_sample(qm_s, cache_mem_k[l].reshape(bd, mem_tokens, mem_w),
                               cache_mem_v[l].reshape(bd, mem_tokens, mem_w)).reshape(ts, mem_w)

        st_p.append((proj[:tp, COL_K:COL_K + kv_w].reshape(b, s, N_KV_HEADS, HEAD_DIM),
                     proj[:tp, COL_V:COL_V + kv_w].reshape(b, s, N_KV_HEADS, HEAD_DIM),
                     proj[:tp, COL_IK:COL_IK + IDX_DIM].reshape(b, s, IDX_DIM),
                     conv_p, h_p, mk_p, mv_p))
        st_s.append((proj_s[:, COL_K:COL_K + kv_w].reshape(bd, tn, N_KV_HEADS, HEAD_DIM),
                     proj_s[:, COL_V:COL_V + kv_w].reshape(bd, tn, N_KV_HEADS, HEAD_DIM),
                     proj_s[:, COL_IK:COL_IK + IDX_DIM].reshape(bd, tn, IDX_DIM),
                     conv_s, h_s))

        cat = lambda p_, s_: _pad_rows(jnp.concatenate([p_, s_]), tm)
        merged = merge(xn, cat(oa_p, oa_s), cat(ol_p, ol_s), cat(om_p, om_s),
                       w_gate[l].astype(BF16), w_br[l].astype(BF16), tm=tm, tn=512)
        x1 = matmul_residual(merged, w_o[l].astype(BF16), x_all, tm=tm, tn=1024)
        qp, xn2 = norm_matmul(x1, g_ffn[l], w_peer_q[l].astype(BF16), tm=tm, tn=1024, xn_dtype=F32)
        ids, gates = peer_route(qp, peer_sub_keys[l].astype(BF16), tm=128)
        x_all = peer_experts(ids, gates, xn2, x1, g_final, peer_u[l], peer_v[l], final_norm=l == depth - 1)
    y_prompt = x_all[:tp].reshape(b, s, d)
    y_sample = x_all[tp:tp + ts].reshape(bd, tn, d)
    k_p, v_p, ik_p, conv_p, h_p, mk_p, mv_p = [jnp.stack(c) for c in zip(*st_p)]
    k_s, v_s, ik_s, conv_s, h_s = [jnp.stack(c) for c in zip(*st_s)]
    return (y_prompt, y_sample, k_p, v_p, ik_p, conv_p, h_p, mk_p, mv_p, k_s, v_s, ik_s, conv_s, h_s)
```
